```python
import math
import jax, jax.numpy as jnp
from jax import lax
import numpy as np

D_MODEL = 4096
BATCH = 4
SEQ = 2048
DEPTH = 2
DEC_BATCH = 8
DEC_SEQ = 8
PAST_LEN = 16384
PAGE_SIZE = 128

N_A_LAYERS = DEPTH // 2
N_B_LAYERS = DEPTH - N_A_LAYERS
SSM_D_INNER = D_MODEL
SSM_HEAD_DIM = 64
SSM_HEADS = SSM_D_INNER // SSM_HEAD_DIM
SSM_GROUPS = 8
SSM_STATE = 128
SSM_CONV = 4
SSM_CHUNK = 128
SSM_CONV_DIM = SSM_D_INNER + 2 * SSM_GROUPS * SSM_STATE
SSM_IN = SSM_D_INNER + SSM_CONV_DIM + SSM_HEADS
ATT_HEAD_DIM = 128
ATT_HEADS = D_MODEL // ATT_HEAD_DIM
KV_HEADS = 8
ATT_GROUP = ATT_HEADS // KV_HEADS
ATT_W = ATT_HEADS * ATT_HEAD_DIM
MOBA_BLOCK = 256
MOBA_TOPK = 3
MOBA_QUERY_ROWS = 16
ATT_SCALE = ATT_HEAD_DIM ** -0.5
REL_BUCKETS = 32
REL_EXACT = REL_BUCKETS // 2
REL_MAX_DIST = 128
MEM_LEN = 256
MEM_HEADS = 4
MEM_HEAD_DIM = D_MODEL // 8
MEM_W = MEM_HEADS * MEM_HEAD_DIM
MEM_SCALE = MEM_HEAD_DIM ** -0.5
PEER_HEADS = 8
PEER_N_KEYS = 128
PEER_EXPERTS = PEER_N_KEYS * PEER_N_KEYS
PEER_TOPK = 16
PEER_KEY_DIM = 256
PEER_ROWS = 128
NORM_EPS = 1e-6

kernel_name = "yoco_mamba2_moba_peer_step"

F32 = jnp.float32


def _rmsnorm(x, g):
    xf = x.astype(F32)
    xf = xf * lax.rsqrt(jnp.mean(xf * xf, axis=-1, keepdims=True) + NORM_EPS)
    return xf.astype(x.dtype) * g


def _segsum(a):
    t = a.shape[-1]
    cs = jnp.cumsum(a, axis=-1)
    d = cs[..., :, None] - cs[..., None, :]
    return jnp.where(jnp.tril(jnp.ones((t, t), bool)), d, -jnp.inf)


def _ssd(x, dt, a_head, bm, cm, h0):
    b, L, H, P = x.shape
    G, N = bm.shape[2], bm.shape[3]
    R = H // G
    q = min(SSM_CHUNK, L)
    nc = -(-L // q)
    pad = nc * q - L
    x = jnp.pad(x.astype(F32), ((0, 0), (0, pad), (0, 0), (0, 0)))
    dt = jnp.pad(dt, ((0, 0), (0, pad), (0, 0)))
    bm = jnp.pad(bm.astype(F32), ((0, 0), (0, pad), (0, 0), (0, 0)))
    cm = jnp.pad(cm.astype(F32), ((0, 0), (0, pad), (0, 0), (0, 0)))
    xdt = (x * dt[..., None]).reshape(b, nc, q, G, R, P)
    a = (dt * a_head).reshape(b, nc, q, G, R).transpose(0, 3, 4, 1, 2)
    bc = bm.reshape(b, nc, q, G, N)
    cc = cm.reshape(b, nc, q, G, N)
    a_cs = jnp.cumsum(a, axis=-1)
    lmat = jnp.exp(_segsum(a))
    cb = jnp.einsum('bcqgn,bcsgn->bgcqs', cc, bc)
    y_diag = jnp.einsum('bgrcqs,bcsgrp->bcqgrp', cb[:, :, None] * lmat, xdt)
    decay_s = jnp.exp(a_cs[..., -1:] - a_cs)
    st = jnp.einsum('bcsgn,bgrcs,bcsgrp->bcgrpn', bc, decay_s, xdt)
    st = jnp.concatenate([h0.astype(F32).reshape(b, 1, G, R, P, N), st], axis=1)
    a_end = jnp.pad(a_cs[..., -1], ((0, 0), (0, 0), (0, 0), (1, 0)))
    dchunk = jnp.exp(_segsum(a_end))
    st_all = jnp.einsum('bgrzc,bcgrpn->bzgrpn', dchunk, st)
    y_off = jnp.einsum('bcqgn,bcgrpn,bgrcq->bcqgrp', cc, st_all[:, :-1], jnp.exp(a_cs))
    y = (y_diag + y_off).reshape(b, nc * q, H, P)[:, :L]
    return y, st_all[:, -1].reshape(b, H, P, N)


def _mamba(zxbcdt, conv_state, ssm_state, conv_w, conv_b, dt_bias, a_log, d_skip, norm_g):
    b, L, _ = zxbcdt.shape
    dtype = zxbcdt.dtype
    z = zxbcdt[..., :SSM_D_INNER]
    xbc = zxbcdt[..., SSM_D_INNER:SSM_D_INNER + SSM_CONV_DIM]
    dt_raw = zxbcdt[..., SSM_D_INNER + SSM_CONV_DIM:]
    full = jnp.concatenate([conv_state.astype(dtype), xbc], axis=1)
    conv = conv_b + sum(full[:, j:j + L] * conv_w[j] for j in range(SSM_CONV))
    xbc = jax.nn.silu(conv)
    new_conv = full[:, L:]
    gn = SSM_GROUPS * SSM_STATE
    xs = xbc[..., :SSM_D_INNER].reshape(b, L, SSM_HEADS, SSM_HEAD_DIM)
    bm = xbc[..., SSM_D_INNER:SSM_D_INNER + gn].reshape(b, L, SSM_GROUPS, SSM_STATE)
    cm = xbc[..., SSM_D_INNER + gn:].reshape(b, L, SSM_GROUPS, SSM_STATE)
    dt = jax.nn.softplus(dt_raw.astype(F32) + dt_bias.astype(F32))
    a_head = -jnp.exp(a_log.astype(F32))
    y, h = _ssd(xs, dt, a_head, bm, cm, ssm_state)
    y = y + xs.astype(F32) * d_skip.astype(F32)[:, None]
    y = y.reshape(b, L, SSM_D_INNER) * jax.nn.silu(z.astype(F32))
    yg = y.reshape(b, L, SSM_GROUPS, SSM_D_INNER // SSM_GROUPS)
    yg = yg * lax.rsqrt(jnp.mean(yg * yg, axis=-1, keepdims=True) + NORM_EPS)
    y = yg.reshape(b, L, SSM_D_INNER) * norm_g.astype(F32)
    return y.astype(dtype), new_conv, h


def _rel_bucket(rel):
    n = jnp.maximum(-rel, 0)
    nf = jnp.maximum(n, 1).astype(F32)
    large = REL_EXACT + (jnp.log(nf / REL_EXACT) / math.log(REL_MAX_DIST / REL_EXACT)
                         * (REL_BUCKETS - REL_EXACT)).astype(jnp.int32)
    return jnp.where(n < REL_EXACT, n, jnp.minimum(large, REL_BUCKETS - 1))


def _moba(q, k, v, q_off, rel_bias):
    b, lq = q.shape[0], q.shape[1]
    lk = k.shape[1]
    qc = max(1, MOBA_QUERY_ROWS // b)
    nq = -(-lq // qc)
    lq_pad = nq * qc
    lk_pad = -(-(q_off + lq_pad) // MOBA_BLOCK) * MOBA_BLOCK
    nblk = lk_pad // MOBA_BLOCK
    q = jnp.pad(q, ((0, 0), (0, lq_pad - lq), (0, 0), (0, 0)))
    k = jnp.pad(k, ((0, 0), (0, lk_pad - lk), (0, 0), (0, 0)))
    v = jnp.pad(v, ((0, 0), (0, lk_pad - lk), (0, 0), (0, 0)))
    kb = k.reshape(b, nblk, MOBA_BLOCK, KV_HEADS, ATT_HEAD_DIM).transpose(0, 3, 1, 2, 4)
    vb = v.reshape(b, nblk, MOBA_BLOCK, KV_HEADS, ATT_HEAD_DIM).transpose(0, 3, 1, 2, 4)
    kmean = jnp.mean(kb.astype(F32), axis=3)
    qg = q.reshape(b, lq_pad, KV_HEADS, ATT_GROUP, ATT_HEAD_DIM)
    qpos = q_off + jnp.arange(lq_pad, dtype=jnp.int32)
    cur = qpos // MOBA_BLOCK
    gate = jnp.einsum('btkgd,bknd->btkgn', qg.astype(F32), kmean)
    past = jnp.arange(nblk)[None, :] < cur[:, None]
    gate = jnp.where(past[None, :, None, None, :], gate, -jnp.inf)
    n_top = min(MOBA_TOPK, nblk)
    top_val, top_idx = lax.top_k(gate, n_top)
    lead = top_idx.shape[:-1]
    sel = jnp.concatenate([top_idx.astype(jnp.int32),
                           jnp.broadcast_to(cur[None, :, None, None, None], lead + (1,))], axis=-1)
    ok = jnp.concatenate([jnp.isfinite(top_val), jnp.ones(lead + (1,), bool)], axis=-1)
    n_sel = n_top + 1

    def chunks(a):
        return jnp.moveaxis(a.reshape((b, nq, qc) + a.shape[2:]), 1, 0)

    bias_tab = rel_bias.T.reshape(KV_HEADS, ATT_GROUP, REL_BUCKETS)
    bi = jnp.arange(b)[:, None, None, None, None]
    ki = jnp.arange(KV_HEADS)[None, None, :, None, None]
    k6 = ki[..., None]
    g6 = jnp.arange(ATT_GROUP)[None, None, None, :, None, None]
    offs = jnp.arange(MOBA_BLOCK, dtype=jnp.int32)

    def step(args):
        qs, ss, oks, ps = args
        kg = kb[bi, ki, ss]
        vg = vb[bi, ki, ss]
        kpos = ss[..., None] * MOBA_BLOCK + offs
        rel = kpos - ps[None, :, None, None, None, None]
        s = jnp.einsum('bqkgd,bqkgjsd->bqkgjs', qs, kg, preferred_element_type=F32) * ATT_SCALE
        s = s + bias_tab[k6, g6, _rel_bucket(rel)].astype(F32)
        s = jnp.where(oks[..., None] & (rel <= 0), s, -jnp.inf)
        p = jax.nn.softmax(s.reshape(s.shape[:4] + (n_sel * MOBA_BLOCK,)), axis=-1)
        vflat = vg.reshape(vg.shape[:4] + (n_sel * MOBA_BLOCK, ATT_HEAD_DIM))
        return jnp.einsum('bqkgs,bqkgsd->bqkgd', p.astype(vg.dtype), vflat)

    out = lax.map(step, (chunks(qg), chunks(sel), chunks(ok), qpos.reshape(nq, qc)))
    out = jnp.moveaxis(out, 0, 1).reshape(b, lq_pad, ATT_HEADS, ATT_HEAD_DIM)
    return out[:, :lq]


def _mem_kv(mem, g, w_k, w_v):
    b, m, _ = mem.shape
    mn = _rmsnorm(mem, g)
    return ((mn @ w_k).reshape(b, m, MEM_HEADS, MEM_HEAD_DIM),
            (mn @ w_v).reshape(b, m, MEM_HEADS, MEM_HEAD_DIM))


def _mem_attend(q, mk, mv):
    s = jnp.einsum('blhd,bmhd->bhlm', q, mk.astype(q.dtype), preferred_element_type=F32) * MEM_SCALE
    p = jax.nn.softmax(s, axis=-1)
    o = jnp.einsum('bhlm,bmhd->blhd', p.astype(q.dtype), mv.astype(q.dtype))
    return o.reshape(q.shape[0], q.shape[1], MEM_W)


def _peer(xn, w_q, keys, u_tab, v_tab):
    b, L, d = xn.shape
    t = b * L
    xt = xn.reshape(t, d)
    q = (xt @ w_q).astype(F32).reshape(t, PEER_HEADS, PEER_KEY_DIM)
    half = PEER_KEY_DIM // 2
    s1 = jnp.einsum('thd,hkd->thk', q[..., :half], keys[:, 0].astype(F32))
    s2 = jnp.einsum('thd,hkd->thk', q[..., half:], keys[:, 1].astype(F32))
    v1, i1 = lax.top_k(s1, PEER_TOPK)
    v2, i2 = lax.top_k(s2, PEER_TOPK)
    cand = (v1[..., :, None] + v2[..., None, :]).reshape(t, PEER_HEADS, PEER_TOPK * PEER_TOPK)
    vals, flat = lax.top_k(cand, PEER_TOPK)
    e = (jnp.take_along_axis(i1, flat // PEER_TOPK, axis=-1) * PEER_N_KEYS
         + jnp.take_along_axis(i2, flat % PEER_TOPK, axis=-1))
    g = jax.nn.softmax(vals, axis=-1)
    ne = PEER_HEADS * PEER_TOPK
    e = e.reshape(t, ne)
    g = g.reshape(t, ne)
    rows = min(PEER_ROWS, t)
    nb = -(-t // rows)
    pad = nb * rows - t
    xt_p = jnp.pad(xt, ((0, pad), (0, 0))).reshape(nb, rows, d)
    e_p = jnp.pad(e, ((0, pad), (0, 0))).reshape(nb, rows, ne)
    g_p = jnp.pad(g, ((0, pad), (0, 0))).reshape(nb, rows, ne)

    def step(args):
        xb, eb, gb = args
        act = jax.nn.gelu(jnp.einsum('td,ted->te', xb, u_tab[eb], preferred_element_type=F32), approximate=False)
        return jnp.einsum('te,ted->td', (gb * act).astype(xb.dtype), v_tab[eb])

    out = lax.map(step, (xt_p, e_p, g_p))
    return out.reshape(nb * rows, d)[:t].reshape(b, L, d)


def setup_inputs(seed: int = 0) -> dict:
    key = jax.random.key(seed)
    ks = iter(jax.random.split(key, 48))

    def nrm(shape, scale):
        return scale * jax.random.normal(next(ks), shape, F32)

    def gain(shape):
        return 1.0 + nrm(shape, 0.01)

    n_pages = PAST_LEN // PAGE_SIZE
    n_phys = (DEC_BATCH * n_pages * 5) // 4
    kvw = KV_HEADS * ATT_HEAD_DIM
    x_prompt = nrm((BATCH, SEQ, D_MODEL), 1.0)
    x_sample = nrm((DEC_BATCH, DEC_SEQ, D_MODEL), 1.0)
    mem_prompt = nrm((BATCH, MEM_LEN, D_MODEL), 1.0)
    cache_k = nrm((n_phys, PAGE_SIZE, KV_HEADS, ATT_HEAD_DIM), 1.0)
    cache_v = nrm((n_phys, PAGE_SIZE, KV_HEADS, ATT_HEAD_DIM), 1.0)
    page_table = jax.random.permutation(next(ks), n_phys)[:DEC_BATCH * n_pages].reshape(
        DEC_BATCH, n_pages).astype(jnp.int32)
    state_ssm = nrm((N_A_LAYERS, DEC_BATCH, SSM_HEADS, SSM_HEAD_DIM, SSM_STATE), 0.5)
    state_conv = nrm((N_A_LAYERS, DEC_BATCH, SSM_CONV - 1, SSM_CONV_DIM), 1.0)
    cache_mem_k = nrm((DEPTH, DEC_BATCH, MEM_LEN, MEM_HEADS, MEM_HEAD_DIM), 1.0)
    cache_mem_v = nrm((DEPTH, DEC_BATCH, MEM_LEN, MEM_HEADS, MEM_HEAD_DIM), 1.0)
    dt0 = jnp.exp(jax.random.uniform(next(ks), (N_A_LAYERS, SSM_HEADS), F32,
                                     minval=math.log(1e-3), maxval=math.log(1e-1)))
    a_dt_bias = dt0 + jnp.log(-jnp.expm1(-dt0))
    a_A_log = jnp.log(jax.random.uniform(next(ks), (N_A_LAYERS, SSM_HEADS), F32, minval=1.0, maxval=16.0))
    return {
        "x_prompt": x_prompt,
        "x_sample": x_sample,
        "mem_prompt": mem_prompt,
        "cache_k": cache_k,
        "cache_v": cache_v,
        "page_table": page_table,
        "state_ssm": state_ssm,
        "state_conv": state_conv,
        "cache_mem_k": cache_mem_k,
        "cache_mem_v": cache_mem_v,
        "norm_mix_g": gain((DEPTH, D_MODEL)),
        "norm_ffn_g": gain((DEPTH, D_MODEL)),
        "final_norm_g": gain((D_MODEL,)),
        "a_w_in": nrm((N_A_LAYERS, D_MODEL, SSM_IN + MEM_W), D_MODEL ** -0.5),
        "a_conv_w": nrm((N_A_LAYERS, SSM_CONV, SSM_CONV_DIM), 0.5),
        "a_conv_b": nrm((N_A_LAYERS, SSM_CONV_DIM), 0.01),
        "a_dt_bias": a_dt_bias,
        "a_A_log": a_A_log,
        "a_D": gain((N_A_LAYERS, SSM_HEADS)),
        "a_norm_g": gain((N_A_LAYERS, SSM_D_INNER)),
        "a_w_out": nrm((N_A_LAYERS, SSM_D_INNER + MEM_W, D_MODEL), (SSM_D_INNER + MEM_W) ** -0.5),
        "kv_norm_g": gain((D_MODEL,)),
        "w_k": nrm((D_MODEL, kvw), D_MODEL ** -0.5),
        "w_v": nrm((D_MODEL, kvw), D_MODEL ** -0.5),
        "rel_bias": nrm((REL_BUCKETS, ATT_HEADS), 0.5),
        "b_w_in": nrm((N_B_LAYERS, D_MODEL, ATT_W + MEM_W), D_MODEL ** -0.5),
        "b_w_out": nrm((N_B_LAYERS, ATT_W + MEM_W, D_MODEL), (ATT_W + MEM_W) ** -0.5),
        "mem_norm_g": gain((DEPTH, D_MODEL)),
        "w_mem_k": nrm((DEPTH, D_MODEL, MEM_W), D_MODEL ** -0.5),
        "w_mem_v": nrm((DEPTH, D_MODEL, MEM_W), D_MODEL ** -0.5),
        "peer_w_q": nrm((DEPTH, D_MODEL, PEER_HEADS * PEER_KEY_DIM), D_MODEL ** -0.5),
        "peer_keys": nrm((DEPTH, PEER_HEADS, 2, PEER_N_KEYS, PEER_KEY_DIM // 2), (PEER_KEY_DIM // 2) ** -0.5),
        "peer_u": nrm((DEPTH, PEER_EXPERTS, D_MODEL), D_MODEL ** -0.5),
        "peer_v": nrm((DEPTH, PEER_EXPERTS, D_MODEL), PEER_HEADS ** -0.5),
    }


def reference(x_prompt, x_sample, mem_prompt, cache_k, cache_v, page_table, state_ssm, state_conv,
              cache_mem_k, cache_mem_v, norm_mix_g, norm_ffn_g, final_norm_g, a_w_in, a_conv_w, a_conv_b,
              a_dt_bias, a_A_log, a_D, a_norm_g, a_w_out, kv_norm_g, w_k, w_v, rel_bias, b_w_in, b_w_out,
              mem_norm_g, w_mem_k, w_mem_v, peer_w_q, peer_keys, peer_u, peer_v):

    def run(x, q_off, mem_k, mem_v, conv0, ssm0, k_past, v_past):
        b, L, _ = x.shape
        convs, ssms = [], []
        k_all = v_all = k_new = v_new = None
        for l in range(DEPTH):
            h = _rmsnorm(x, norm_mix_g[l])
            if l < N_A_LAYERS:
                p = h @ a_w_in[l]
                y_tok, c_new, s_new = _mamba(p[..., :SSM_IN], conv0[l], ssm0[l], a_conv_w[l], a_conv_b[l],
                                             a_dt_bias[l], a_A_log[l], a_D[l], a_norm_g[l])
                convs.append(c_new)
                ssms.append(s_new)
                qm = p[..., SSM_IN:]
                w_out = a_w_out[l]
            else:
                j = l - N_A_LAYERS
                p = h @ b_w_in[j]
                q = p[..., :ATT_W].reshape(b, L, ATT_HEADS, ATT_HEAD_DIM)
                y_tok = _moba(q, k_all, v_all, q_off, rel_bias).reshape(b, L, ATT_W)
                qm = p[..., ATT_W:]
                w_out = b_w_out[j]
            o_mem = _mem_attend(qm.reshape(b, L, MEM_HEADS, MEM_HEAD_DIM), mem_k[l], mem_v[l])
            x = x + jnp.concatenate([y_tok, o_mem], axis=-1) @ w_out
            x = x + _peer(_rmsnorm(x, norm_ffn_g[l]), peer_w_q[l], peer_keys[l], peer_u[l], peer_v[l])
            if l == N_A_LAYERS - 1:
                kv_in = _rmsnorm(x, kv_norm_g)
                k_new = (kv_in @ w_k).reshape(b, L, KV_HEADS, ATT_HEAD_DIM)
                v_new = (kv_in @ w_v).reshape(b, L, KV_HEADS, ATT_HEAD_DIM)
                if k_past is None:
                    k_all, v_all = k_new, v_new
                else:
                    k_all = jnp.concatenate([k_past.astype(k_new.dtype), k_new], axis=1)
                    v_all = jnp.concatenate([v_past.astype(v_new.dtype), v_new], axis=1)
        return _rmsnorm(x, final_norm_g), jnp.stack(convs), jnp.stack(ssms), k_new, v_new

    bp = x_prompt.shape[0]
    mem_kv_p = [_mem_kv(mem_prompt, mem_norm_g[l], w_mem_k[l], w_mem_v[l]) for l in range(DEPTH)]
    mem_k_prompt = jnp.stack([m[0] for m in mem_kv_p])
    mem_v_prompt = jnp.stack([m[1] for m in mem_kv_p])
    conv0 = jnp.zeros((N_A_LAYERS, bp, SSM_CONV - 1, SSM_CONV_DIM), x_prompt.dtype)
    ssm0 = jnp.zeros((N_A_LAYERS, bp, SSM_HEADS, SSM_HEAD_DIM, SSM_STATE), F32)
    y_prompt, state_conv_prompt, state_ssm_prompt, k_prompt, v_prompt = run(
        x_prompt, 0, mem_k_prompt, mem_v_prompt, conv0, ssm0, None, None)

    db, n_pages = page_table.shape
    past_len = n_pages * cache_k.shape[1]
    k_past = cache_k[page_table].reshape(db, past_len, KV_HEADS, ATT_HEAD_DIM)
    v_past = cache_v[page_table].reshape(db, past_len, KV_HEADS, ATT_HEAD_DIM)
    y_sample, state_conv_sample, state_ssm_sample, k_sample, v_sample = run(
        x_sample, past_len, cache_mem_k, cache_mem_v, state_conv, state_ssm, k_past, v_past)

    return (y_prompt, y_sample, state_ssm_prompt, state_conv_prompt, k_prompt, v_prompt, mem_k_prompt,
            mem_v_prompt, state_ssm_sample, state_conv_sample, k_sample, v_sample)
```

```python
import functools
import math

import numpy as np
import jax
import jax.numpy as jnp
from jax import lax
from jax.experimental import pallas as pl
from jax.experimental.pallas import tpu as pltpu

F32 = jnp.float32
BF16 = jnp.bfloat16
HIGHEST = lax.Precision.HIGHEST

D_MODEL = 4096
NORM_EPS = 1e-6
SSM_HEADS = 64
SSM_HEAD_DIM = 64
SSM_GROUPS = 8
SSM_GROUP_W = D_MODEL // SSM_GROUPS
SSM_STATE = 128
SSM_CONV = 4
SSM_CHUNK = 128
SSM_BC_W = 2 * SSM_GROUPS * SSM_STATE
SSM_CONV_DIM = D_MODEL + SSM_BC_W
SSM_ZX_W = D_MODEL + SSM_CONV_DIM
ATT_HEAD_DIM = 128
ATT_HEADS = 32
KV_HEADS = 8
ATT_GROUP = ATT_HEADS // KV_HEADS
KV_W = KV_HEADS * ATT_HEAD_DIM
MOBA_BLOCK = 256
MOBA_TOPK = 3
ATT_SCALE = ATT_HEAD_DIM ** -0.5
REL_BUCKETS = 32
REL_EXACT = 16
REL_MAX_DIST = 128
MEM_LEN = 256
MEM_HEADS = 4
MEM_HEAD_DIM = 512
MEM_W = MEM_HEADS * MEM_HEAD_DIM
MEM_SCALE = MEM_HEAD_DIM ** -0.5
PEER_HEADS = 8
PEER_N_KEYS = 128
PEER_EXPERTS = PEER_N_KEYS * PEER_N_KEYS
PEER_TOPK = 16
PEER_KEY_DIM = 256
PEER_HALF = PEER_KEY_DIM // 2

NEG = -1e30
V7X_VMEM_LIMIT = 56 * 1024 * 1024


def _params(*sem):
    return pltpu.CompilerParams(dimension_semantics=sem, vmem_limit_bytes=V7X_VMEM_LIMIT)


def _nt(a, b, precision=None):
    return lax.dot_general(a, b, (((1,), (1,)), ((), ())), preferred_element_type=F32, precision=precision)


def _tn(a, b, precision=None):
    return lax.dot_general(a, b, (((0,), (0,)), ((), ())), preferred_element_type=F32, precision=precision)


def _row_tile(m, pref):
    return pref if m % pref == 0 else m


def _add_norm_kernel(*refs, n_in, n_gain, emit_sum):
    ins = refs[:n_in]
    g_ref = refs[n_in]
    outs = refs[n_in + 1:]
    x = ins[0][...]
    for r in ins[1:]:
        x = x + r[...]
    k = 0
    if emit_sum:
        outs[0][...] = x
        k = 1
    xn = x * lax.rsqrt(jnp.mean(x * x, axis=-1, keepdims=True) + NORM_EPS)
    for j in range(n_gain):
        outs[k + j][...] = (xn * g_ref[j:j + 1, :]).astype(outs[k + j].dtype)


def add_norm(xs, gains, out_dtypes, emit_sum):
    m, d = xs[0].shape
    tm = _row_tile(m, 256)
    g = jnp.stack(gains).astype(F32)
    row = pl.BlockSpec((tm, d), lambda i: (i, 0))
    out_shape = ([jax.ShapeDtypeStruct((m, d), F32)] if emit_sum else []) + [
        jax.ShapeDtypeStruct((m, d), dt) for dt in out_dtypes]
    return pl.pallas_call(
        functools.partial(_add_norm_kernel, n_in=len(xs), n_gain=len(gains), emit_sum=emit_sum),
        grid=(m // tm,),
        in_specs=[row] * len(xs) + [pl.BlockSpec((len(gains), d), lambda i: (0, 0))],
        out_specs=[row] * len(out_shape),
        out_shape=out_shape,
        compiler_params=_params("parallel"),
        name="add_norm",
    )(*xs, g)


def _matmul_kernel(*refs, n_x, has_res):
    xs = refs[:n_x]
    ws = refs[n_x:2 * n_x]
    o_ref = refs[-1]
    acc = jnp.dot(xs[0][...], ws[0][...].astype(BF16), preferred_element_type=F32)
    for x_ref, w_ref in zip(xs[1:], ws[1:]):
        acc = acc + jnp.dot(x_ref[...], w_ref[...].astype(BF16), preferred_element_type=F32)
    if has_res:
        acc = acc + refs[2 * n_x][...]
    o_ref[...] = acc


def matmul(xs, w, col0, n, res=None, tm=512, tn=512):
    m = xs[0].shape[0]
    tm = _row_tile(m, tm)
    tn = min(tn, n)
    assert n % tn == 0 and col0 % tn == 0
    cb = col0 // tn
    in_specs, row0 = [], 0
    for x in xs:
        in_specs.append(pl.BlockSpec((tm, x.shape[1]), lambda i, j: (i, 0)))
    for x in xs:
        k = x.shape[1]
        assert row0 % k == 0
        in_specs.append(pl.BlockSpec((k, tn), functools.partial(lambda i, j, rb: (rb, j + cb), rb=row0 // k)))
        row0 += k
    assert row0 == w.shape[0]
    args = list(xs) + [w] * len(xs)
    if res is not None:
        in_specs.append(pl.BlockSpec((tm, tn), lambda i, j: (i, j)))
        args.append(res)
    return pl.pallas_call(
        functools.partial(_matmul_kernel, n_x=len(xs), has_res=res is not None),
        grid=(m // tm, n // tn),
        in_specs=in_specs,
        out_specs=pl.BlockSpec((tm, tn), lambda i, j: (i, j)),
        out_shape=jax.ShapeDtypeStruct((m, n), F32),
        compiler_params=_params("parallel", "arbitrary"),
        name="matmul",
    )(*args)


def _mem_attn_kernel(q_ref, k_ref, v_ref, o_ref):
    for h in range(MEM_HEADS):
        sl = slice(h * MEM_HEAD_DIM, (h + 1) * MEM_HEAD_DIM)
        q = q_ref[:, sl].astype(BF16)
        s = _nt(q, k_ref[0, :, sl].astype(BF16)) * MEM_SCALE
        p = jnp.exp(s - jnp.max(s, axis=-1, keepdims=True))
        o = jnp.dot(p.astype(BF16), v_ref[0, :, sl].astype(BF16), preferred_element_type=F32)
        o_ref[:, sl] = (o / jnp.sum(p, axis=-1, keepdims=True)).astype(o_ref.dtype)


def mem_attn(p, col0, mem_k, mem_v, seq):
    m = p.shape[0]
    nb = m // seq
    tq = _row_tile(seq, 512)
    nt = seq // tq
    cb = col0 // MEM_W
    kv_spec = pl.BlockSpec((1, MEM_LEN, MEM_W), lambda b, i: (b, 0, 0))
    return pl.pallas_call(
        _mem_attn_kernel,
        grid=(nb, nt),
        in_specs=[pl.BlockSpec((tq, MEM_W), lambda b, i: (b * nt + i, cb)), kv_spec, kv_spec],
        out_specs=pl.BlockSpec((tq, MEM_W), lambda b, i: (b * nt + i, 0)),
        out_shape=jax.ShapeDtypeStruct((m, MEM_W), BF16),
        compiler_params=_params("parallel", "arbitrary"),
        name="mem_attn",
    )(p, mem_k.reshape(nb, MEM_LEN, MEM_W), mem_v.reshape(nb, MEM_LEN, MEM_W))


def _silu(x):
    return x * jax.nn.sigmoid(x)


def _mamba_kernel(z_ref, x_ref, bc_ref, dt_ref, conv0_ref, h0_ref, cw_ref, cb_ref, dtb_ref, alog_ref, dexp_ref,
                  ng_ref, y_ref, conv_out_ref, h_out_ref, xpad, state, *, q):
    c = pl.program_id(1)
    nc = pl.num_programs(1)
    tail = SSM_CONV - 1
    base = 8 - tail

    @pl.when(c == 0)
    def _():
        xpad[base:8, :] = conv0_ref[0]
        state[...] = h0_ref[0]

    xpad[8:8 + q, :D_MODEL] = x_ref[...]
    xpad[8:8 + q, D_MODEL:] = bc_ref[...]
    conv = cb_ref[...] + xpad[base:base + q, :] * cw_ref[0:1, :]
    for j in range(1, SSM_CONV):
        conv = conv + xpad[base + j:base + j + q, :] * cw_ref[j:j + 1, :]
    new_tail = xpad[8 + q - tail:8 + q, :]

    @pl.when(c == nc - 1)
    def _():
        conv_out_ref[0] = new_tail

    xpad[base:8, :] = new_tail
    xbc = _silu(conv)
    xs = xbc[:, :D_MODEL]

    dt_in = dt_ref[...] + dtb_ref[...]
    dt = jnp.maximum(dt_in, 0.0) + jnp.log1p(jnp.exp(-jnp.abs(dt_in)))
    a = dt * (-jnp.exp(alog_ref[...]))
    ti = lax.broadcasted_iota(jnp.int32, (q, q), 0)
    si = lax.broadcasted_iota(jnp.int32, (q, q), 1)
    causal = ti >= si
    cs = jnp.dot(causal.astype(F32), a, preferred_element_type=F32, precision=HIGHEST)
    eye = (lax.broadcasted_iota(jnp.int32, (SSM_HEADS, SSM_HEADS), 0)
           == lax.broadcasted_iota(jnp.int32, (SSM_HEADS, SSM_HEADS), 1)).astype(F32)
    cs_t = _nt(eye, cs, precision=HIGHEST)
    cs_end = cs[q - 1:q, :]
    hh = lax.broadcasted_iota(jnp.int32, (SSM_HEADS, D_MODEL), 0)
    ch = lax.broadcasted_iota(jnp.int32, (SSM_HEADS, D_MODEL), 1)
    expand = (ch // SSM_HEAD_DIM == hh).astype(F32)
    per_head = jnp.concatenate([dt, jnp.exp(cs), jnp.exp(cs_end - cs)], axis=0)
    rep = jnp.dot(per_head, expand, preferred_element_type=F32, precision=HIGHEST)
    dt_rep, in_decay, out_decay = rep[:q], rep[q:2 * q], rep[2 * q:]
    xdt = xs * dt_rep
    xw = (xdt * out_decay).astype(BF16)
    xdt_b = xdt.astype(BF16)
    lane = lax.broadcasted_iota(jnp.int32, (q, 2 * SSM_HEAD_DIM), 1)
    left = lane < SSM_HEAD_DIM
    zero = jnp.zeros((), BF16)
    heads_per_group = SSM_HEADS // SSM_GROUPS

    for g in range(SSM_GROUPS):
        bm = xbc[:, D_MODEL + g * SSM_STATE:D_MODEL + (g + 1) * SSM_STATE].astype(BF16)
        cm = xbc[:, D_MODEL + SSM_GROUPS * SSM_STATE + g * SSM_STATE:
                 D_MODEL + SSM_GROUPS * SSM_STATE + (g + 1) * SSM_STATE].astype(BF16)
        gs = slice(g * SSM_GROUP_W, (g + 1) * SSM_GROUP_W)
        cb = _nt(cm, bm)
        s_in = state[g]
        y_g = jnp.dot(cm, s_in.astype(BF16), preferred_element_type=F32) * in_decay[:, gs]
        pieces = []
        for pr in range(heads_per_group // 2):
            h0 = g * heads_per_group + 2 * pr
            ps = slice(g * SSM_GROUP_W + pr * 2 * SSM_HEAD_DIM, g * SSM_GROUP_W + (pr + 1) * 2 * SSM_HEAD_DIM)
            xp = xdt_b[:, ps]
            acc = None
            for k, keep in ((0, left), (1, jnp.logical_not(left))):
                h = h0 + k
                seg = cs[:, h:h + 1] - cs_t[h:h + 1, :]
                lmat = jnp.where(causal, jnp.exp(jnp.minimum(seg, 0.0)), 0.0)
                part = jnp.dot((cb * lmat).astype(BF16), jnp.where(keep, xp, zero), preferred_element_type=F32)
                acc = part if acc is None else acc + part
            pieces.append(acc)
        y_g = y_g + jnp.concatenate(pieces, axis=1)
        state[g] = s_in * in_decay[q - 1:q, gs] + _tn(bm, xw[:, gs])
        y_g = y_g + xs[:, gs] * dexp_ref[:, gs]
        y_g = y_g * _silu(z_ref[:, gs])
        y_g = y_g * lax.rsqrt(jnp.mean(y_g * y_g, axis=-1, keepdims=True) + NORM_EPS)
        y_ref[:, gs] = (y_g * ng_ref[:, gs]).astype(y_ref.dtype)

    @pl.when(c == nc - 1)
    def _():
        h_out_ref[0] = state[...]


def mamba(zx, dt_raw, conv0, h0, conv_w, conv_b, dt_bias, a_log, d_skip, norm_g, seq):
    m = zx.shape[0]
    nb = m // seq
    q = min(SSM_CHUNK, seq)
    assert seq % q == 0
    nc = seq // q
    rg = SSM_HEADS // SSM_GROUPS
    h0_t = h0.reshape(nb, SSM_GROUPS, rg, SSM_HEAD_DIM, SSM_STATE).transpose(0, 1, 4, 2, 3).reshape(
        nb, SSM_GROUPS, SSM_STATE, SSM_GROUP_W)
    row = lambda w, cbk: pl.BlockSpec((q, w), lambda b, c: (b * nc + c, cbk))
    vec = lambda w: pl.BlockSpec((1, w), lambda b, c: (0, 0))
    y, conv_out, h_t = pl.pallas_call(
        functools.partial(_mamba_kernel, q=q),
        grid=(nb, nc),
        in_specs=[
            row(D_MODEL, 0), row(D_MODEL, 1), row(SSM_BC_W, 2 * D_MODEL // SSM_BC_W),
            pl.BlockSpec((q, SSM_HEADS), lambda b, c: (b * nc + c, 0)),
            pl.BlockSpec((1, SSM_CONV - 1, SSM_CONV_DIM), lambda b, c: (b, 0, 0)),
            pl.BlockSpec((1, SSM_GROUPS, SSM_STATE, SSM_GROUP_W), lambda b, c: (b, 0, 0, 0)),
            pl.BlockSpec((SSM_CONV, SSM_CONV_DIM), lambda b, c: (0, 0)),
            vec(SSM_CONV_DIM), vec(SSM_HEADS), vec(SSM_HEADS), vec(D_MODEL), vec(D_MODEL),
        ],
        out_specs=[
            pl.BlockSpec((q, D_MODEL), lambda b, c: (b * nc + c, 0)),
            pl.BlockSpec((1, SSM_CONV - 1, SSM_CONV_DIM), lambda b, c: (b, 0, 0)),
            pl.BlockSpec((1, SSM_GROUPS, SSM_STATE, SSM_GROUP_W), lambda b, c: (b, 0, 0, 0)),
        ],
        out_shape=[
            jax.ShapeDtypeStruct((m, D_MODEL), BF16),
            jax.ShapeDtypeStruct((nb, SSM_CONV - 1, SSM_CONV_DIM), F32),
            jax.ShapeDtypeStruct((nb, SSM_GROUPS, SSM_STATE, SSM_GROUP_W), F32),
        ],
        scratch_shapes=[pltpu.VMEM((8 + q, SSM_CONV_DIM), F32),
                        pltpu.VMEM((SSM_GROUPS, SSM_STATE, SSM_GROUP_W), F32)],
        compiler_params=_params("parallel", "arbitrary"),
        name="mamba",
    )(zx, zx, zx, dt_raw, conv0, h0_t, conv_w, conv_b.reshape(1, -1), dt_bias.reshape(1, -1),
      a_log.reshape(1, -1), jnp.repeat(d_skip, SSM_HEAD_DIM).reshape(1, -1), norm_g.reshape(1, -1))
    h = h_t.reshape(nb, SSM_GROUPS, SSM_STATE, rg, SSM_HEAD_DIM).transpose(0, 1, 3, 4, 2).reshape(
        nb, SSM_HEADS, SSM_HEAD_DIM, SSM_STATE)
    return y, conv_out, h


def _rel_thresholds():
    n = np.arange(0, 4 * REL_MAX_DIST)

    def buckets(dtype):
        nf = np.maximum(n, 1).astype(dtype)
        large = REL_EXACT + (np.log(nf / dtype(REL_EXACT)) / dtype(math.log(REL_MAX_DIST / REL_EXACT))
                             * dtype(REL_BUCKETS - REL_EXACT)).astype(np.int32)
        return np.where(n < REL_EXACT, n, np.minimum(large, REL_BUCKETS - 1))

    b = buckets(np.float64)
    assert (b == buckets(np.float32)).all() and (np.diff(b) >= 0).all() and b[-1] == REL_BUCKETS - 1
    return [int(np.argmax(b >= k)) for k in range(1, REL_BUCKETS)]


REL_THRESHOLDS = _rel_thresholds()


def _rel_bias(n, table, max_bucket=REL_BUCKETS - 1):
    out = jnp.where(n >= REL_THRESHOLDS[0], table(1), table(0))
    for k in range(2, max_bucket + 1):
        out = jnp.where(n >= REL_THRESHOLDS[k - 1], table(k), out)
    return out


def _moba_bias_kernel(tab_ref, o_ref):
    h = pl.program_id(0)
    t = lax.broadcasted_iota(jnp.int32, (MOBA_BLOCK, MOBA_BLOCK), 0)
    s = lax.broadcasted_iota(jnp.int32, (MOBA_BLOCK, MOBA_BLOCK), 1)
    for d in range(2):
        o_ref[0, d] = _rel_bias(d * MOBA_BLOCK + t - s, lambda k: tab_ref[h, k]).astype(F32)


def moba_bias(rel_bias_t):
    return pl.pallas_call(
        _moba_bias_kernel,
        grid=(ATT_HEADS,),
        in_specs=[pl.BlockSpec(memory_space=pltpu.SMEM)],
        out_specs=pl.BlockSpec((1, 2, MOBA_BLOCK, MOBA_BLOCK), lambda h: (h, 0, 0, 0)),
        out_shape=jax.ShapeDtypeStruct((ATT_HEADS, 2, MOBA_BLOCK, MOBA_BLOCK), F32),
        compiler_params=_params("parallel"),
        name="moba_bias",
    )(rel_bias_t)


def _top_blocks(gate, n_lanes):
    lane = lax.broadcasted_iota(jnp.int32, gate.shape, 1)
    sel = jnp.zeros(gate.shape, F32)
    cur = gate
    for _ in range(MOBA_TOPK):
        mx = jnp.max(cur, axis=-1, keepdims=True)
        idx = jnp.min(jnp.where(cur == mx, lane, n_lanes), axis=-1, keepdims=True)
        hit = lane == idx
        sel = jnp.where(hit & (mx > 0.5 * NEG), 1.0, sel)
        cur = jnp.where(hit, 2.0 * NEG, cur)
    return sel


def _moba_prompt_kernel(tab_ref, q_ref, k_ref, v_ref, bias_ref, o_ref, m_s, l_s, acc_s, *, nblk):
    kv = pl.program_id(0)
    i = pl.program_id(2)
    rows = ATT_GROUP * MOBA_BLOCK
    qf = jnp.concatenate([q_ref[:, g * ATT_HEAD_DIM:(g + 1) * ATT_HEAD_DIM] for g in range(ATT_GROUP)], axis=0)
    qb = qf.astype(BF16)
    kmean = jnp.concatenate(
        [jnp.mean(k_ref[j * MOBA_BLOCK:(j + 1) * MOBA_BLOCK, :], axis=0, keepdims=True) for j in range(nblk)], axis=0)
    gate = _nt(qf, kmean, precision=HIGHEST)
    blk = lax.broadcasted_iota(jnp.int32, (rows, nblk), 1)
    sel = _top_blocks(jnp.where(blk < i, gate, NEG), nblk)

    t = lax.broadcasted_iota(jnp.int32, (rows, MOBA_BLOCK), 0) % MOBA_BLOCK
    s_i = lax.broadcasted_iota(jnp.int32, (rows, MOBA_BLOCK), 1)
    j0 = pl.multiple_of(i * MOBA_BLOCK, MOBA_BLOCK)
    s = _nt(qb, k_ref[pl.ds(j0, MOBA_BLOCK), :].astype(BF16)) * ATT_SCALE
    s = s + jnp.concatenate([bias_ref[g, 0] for g in range(ATT_GROUP)], axis=0)
    s = jnp.where(s_i <= t, s, NEG)
    m0 = jnp.max(s, axis=-1, keepdims=True)
    p = jnp.exp(s - m0)
    m_s[...] = m0
    l_s[...] = jnp.sum(p, axis=-1, keepdims=True)
    acc_s[...] = jnp.dot(p.astype(BF16), v_ref[pl.ds(j0, MOBA_BLOCK), :].astype(BF16), preferred_element_type=F32)

    far = jnp.concatenate(
        [jnp.full((MOBA_BLOCK, 1), tab_ref[kv * ATT_GROUP + g, REL_BUCKETS - 1], F32) for g in range(ATT_GROUP)], axis=0)
    for j in range(nblk - 1):
        @pl.when(j < i)
        def _(j=j):
            s = _nt(qb, k_ref[j * MOBA_BLOCK:(j + 1) * MOBA_BLOCK, :].astype(BF16)) * ATT_SCALE
            near = jnp.concatenate([bias_ref[g, 1] for g in range(ATT_GROUP)], axis=0)
            s = s + jnp.where(i - j == 1, near, far)
            s = jnp.where(sel[:, j:j + 1] > 0.5, s, NEG)
            m_old = m_s[...]
            m_new = jnp.maximum(m_old, jnp.max(s, axis=-1, keepdims=True))
            alpha = jnp.exp(m_old - m_new)
            p = jnp.exp(s - m_new)
            m_s[...] = m_new
            l_s[...] = alpha * l_s[...] + jnp.sum(p, axis=-1, keepdims=True)
            acc_s[...] = alpha * acc_s[...] + jnp.dot(
                p.astype(BF16), v_ref[j * MOBA_BLOCK:(j + 1) * MOBA_BLOCK, :].astype(BF16), preferred_element_type=F32)

    out = acc_s[...] / l_s[...]
    for g in range(ATT_GROUP):
        o_ref[:, g * ATT_HEAD_DIM:(g + 1) * ATT_HEAD_DIM] = out[g * MOBA_BLOCK:(g + 1) * MOBA_BLOCK].astype(o_ref.dtype)


def moba_prompt(p, k_new, v_new, rel_bias_t, seq):
    m = p.shape[0]
    nb = m // seq
    assert seq % MOBA_BLOCK == 0
    nblk = seq // MOBA_BLOCK
    gw = ATT_GROUP * ATT_HEAD_DIM
    bias = moba_bias(rel_bias_t)
    kv_spec = pl.BlockSpec((seq, ATT_HEAD_DIM), lambda kv, b, i: (b, kv))
    rows = ATT_GROUP * MOBA_BLOCK
    return pl.pallas_call(
        functools.partial(_moba_prompt_kernel, nblk=nblk),
        grid=(KV_HEADS, nb, nblk),
        in_specs=[
            pl.BlockSpec(memory_space=pltpu.SMEM),
            pl.BlockSpec((MOBA_BLOCK, gw), lambda kv, b, i: (b * nblk + i, kv)),
            kv_spec, kv_spec,
            pl.BlockSpec((ATT_GROUP, 2, MOBA_BLOCK, MOBA_BLOCK), lambda kv, b, i: (kv, 0, 0, 0)),
        ],
        out_specs=pl.BlockSpec((MOBA_BLOCK, gw), lambda kv, b, i: (b * nblk + i, kv)),
        out_shape=jax.ShapeDtypeStruct((m, ATT_HEADS * ATT_HEAD_DIM), BF16),
        scratch_shapes=[pltpu.VMEM((rows, 1), F32), pltpu.VMEM((rows, 1), F32), pltpu.VMEM((rows, ATT_HEAD_DIM), F32)],
        compiler_params=_params("parallel", "parallel", "arbitrary"),
        name="moba_prompt",
    )(rel_bias_t, p, k_new, v_new, bias)


def _moba_sample_kernel(pt_ref, tab_ref, q_ref, k0_ref, k1_ref, v0_ref, v1_ref, kn_ref, vn_ref, o_ref,
                        gate_s, m_s, l_s, acc_s, *, n_new, nblk):
    del pt_ref
    j = pl.program_id(1)
    rows = ATT_GROUP * n_new
    lanes = gate_s.shape[-1]
    lane = lax.broadcasted_iota(jnp.int32, (rows, lanes), 1)
    row_g = lax.broadcasted_iota(jnp.int32, (rows, 1), 0) // n_new
    row_t = lax.broadcasted_iota(jnp.int32, (rows, 1), 0) % n_new

    def head_col(kv, k):
        col = jnp.full((rows, 1), tab_ref[kv * ATT_GROUP, k], F32)
        for g in range(1, ATT_GROUP):
            col = jnp.where(row_g == g, tab_ref[kv * ATT_GROUP + g, k], col)
        return col

    def q_rows(kv):
        return jnp.concatenate([q_ref[:, (kv * ATT_GROUP + g) * ATT_HEAD_DIM:(kv * ATT_GROUP + g + 1) * ATT_HEAD_DIM]
                                for g in range(ATT_GROUP)], axis=0)

    @pl.when(j == 0)
    def _():
        gate_s[...] = jnp.full(gate_s.shape, NEG, F32)
        m_s[...] = jnp.full(m_s.shape, NEG, F32)
        l_s[...] = jnp.zeros(l_s.shape, F32)

    for kv in range(KV_HEADS):
        ks = slice(kv * ATT_HEAD_DIM, (kv + 1) * ATT_HEAD_DIM)
        qf = q_rows(kv)
        kb = jnp.concatenate([k0_ref[0, :, ks], k1_ref[0, :, ks]], axis=0)
        vb = jnp.concatenate([v0_ref[0, :, ks], v1_ref[0, :, ks]], axis=0)
        gate_col = jnp.sum(qf * jnp.mean(kb, axis=0, keepdims=True), axis=-1, keepdims=True)
        s = _nt(qf.astype(BF16), kb.astype(BF16)) * ATT_SCALE
        s_i = lax.broadcasted_iota(jnp.int32, (rows, MOBA_BLOCK), 1)
        dist = (nblk - j) * MOBA_BLOCK + row_t - s_i
        far = head_col(kv, REL_BUCKETS - 1)
        s = s + lax.cond(j == nblk - 1,
                         lambda: _rel_bias(dist, functools.partial(head_col, kv)) + jnp.zeros_like(s),
                         lambda: far + jnp.zeros_like(s))
        mj = jnp.max(s, axis=-1, keepdims=True)
        p = jnp.exp(s - mj)
        here = lane == j
        gate_s[kv] = jnp.where(here, gate_col, gate_s[kv])
        m_s[kv] = jnp.where(here, mj, m_s[kv])
        l_s[kv] = jnp.where(here, jnp.sum(p, axis=-1, keepdims=True), l_s[kv])
        acc_s[j, kv] = jnp.dot(p.astype(BF16), vb.astype(BF16), preferred_element_type=F32)

    @pl.when(j == nblk - 1)
    def _():
        for kv in range(KV_HEADS):
            ks = slice(kv * ATT_HEAD_DIM, (kv + 1) * ATT_HEAD_DIM)
            qf = q_rows(kv)
            s_i = lax.broadcasted_iota(jnp.int32, (rows, n_new), 1)
            s = _nt(qf.astype(BF16), kn_ref[:, ks].astype(BF16)) * ATT_SCALE
            s = s + _rel_bias(row_t - s_i, functools.partial(head_col, kv), max_bucket=n_new - 1)
            s = jnp.where(s_i <= row_t, s, NEG)
            mc = jnp.max(s, axis=-1, keepdims=True)
            pc = jnp.exp(s - mc)
            lc = jnp.sum(pc, axis=-1, keepdims=True)
            accc = jnp.dot(pc.astype(BF16), vn_ref[:, ks].astype(BF16), preferred_element_type=F32)
            sel = _top_blocks(gate_s[kv], lanes)
            mt = jnp.maximum(jnp.max(jnp.where(sel > 0.5, m_s[kv], NEG), axis=-1, keepdims=True), mc)
            w = jnp.where(sel > 0.5, jnp.exp(m_s[kv] - mt), 0.0)
            wc = jnp.exp(mc - mt)
            den = jnp.sum(w * l_s[kv], axis=-1, keepdims=True) + wc * lc

            def body(jj, num):
                col = jnp.sum(jnp.where(lane == jj, w, 0.0), axis=-1, keepdims=True)
                return num + col * acc_s[jj, kv]

            num = lax.fori_loop(0, nblk, body, wc * accc)
            out = num / den
            for g in range(ATT_GROUP):
                hs = slice((kv * ATT_GROUP + g) * ATT_HEAD_DIM, (kv * ATT_GROUP + g + 1) * ATT_HEAD_DIM)
                o_ref[:, hs] = out[g * n_new:(g + 1) * n_new].astype(o_ref.dtype)


def moba_sample(p, k_new, v_new, cache_k, cache_v, page_table, rel_bias_t, n_new):
    m = p.shape[0]
    nb = m // n_new
    n_phys, page, _, _ = cache_k.shape
    assert MOBA_BLOCK == 2 * page and n_new < REL_EXACT and n_new <= MOBA_BLOCK
    nblk = page_table.shape[1] // 2
    lanes = -(-nblk // 128) * 128
    ck = cache_k.reshape(n_phys, page, KV_W)
    cv = cache_v.reshape(n_phys, page, KV_W)
    page_spec = lambda half: pl.BlockSpec((1, page, KV_W), lambda b, j, pt: (pt[b, 2 * j + half], 0, 0))
    new_spec = pl.BlockSpec((n_new, KV_W), lambda b, j, pt: (b, 0))
    rows = ATT_GROUP * n_new
    return pl.pallas_call(
        functools.partial(_moba_sample_kernel, n_new=n_new, nblk=nblk),
        grid_spec=pltpu.PrefetchScalarGridSpec(
            num_scalar_prefetch=1,
            grid=(nb, nblk),
            in_specs=[
                pl.BlockSpec(memory_space=pltpu.SMEM),
                pl.BlockSpec((n_new, ATT_HEADS * ATT_HEAD_DIM), lambda b, j, pt: (b, 0)),
                page_spec(0), page_spec(1), page_spec(0), page_spec(1), new_spec, new_spec,
            ],
            out_specs=pl.BlockSpec((n_new, ATT_HEADS * ATT_HEAD_DIM), lambda b, j, pt: (b, 0)),
            scratch_shapes=[pltpu.VMEM((KV_HEADS, rows, lanes), F32), pltpu.VMEM((KV_HEADS, rows, lanes), F32),
                            pltpu.VMEM((KV_HEADS, rows, lanes), F32),
                            pltpu.VMEM((nblk, KV_HEADS, rows, ATT_HEAD_DIM), F32)],
        ),
        out_shape=jax.ShapeDtypeStruct((m, ATT_HEADS * ATT_HEAD_DIM), BF16),
        compiler_params=_params("parallel", "arbitrary"),
        name="moba_sample",
    )(page_table, rel_bias_t, p, ck, ck, cv, cv, k_new, v_new)


def _top_values(x, n):
    r = lax.broadcasted_iota(jnp.int32, x.shape, 0)
    out = []
    for _ in range(n):
        mx = jnp.max(x, axis=0, keepdims=True)
        out.append(mx)
        first = jnp.min(jnp.where(x == mx, r, x.shape[0]), axis=0, keepdims=True)
        x = jnp.where(r == first, -jnp.inf, x)
    return jnp.concatenate(out, axis=0)


def _peer_route_kernel(q_ref, keys_ref, s1_ref, s2_ref, stat_ref):
    tm = q_ref.shape[0]
    for h in range(PEER_HEADS):
        c0 = h * PEER_KEY_DIM
        s1 = _nt(keys_ref[h, 0], q_ref[:, c0:c0 + PEER_HALF], precision=HIGHEST)
        s2 = _nt(keys_ref[h, 1], q_ref[:, c0 + PEER_HALF:c0 + PEER_KEY_DIM], precision=HIGHEST)
        v1 = _top_values(s1, PEER_TOPK)
        v2 = _top_values(s2, PEER_TOPK)
        cand = jnp.concatenate([v1[a:a + 1] + v2 for a in range(PEER_TOPK)], axis=0)
        top = _top_values(cand, PEER_TOPK)
        z = jnp.sum(jnp.exp(top - top[0:1]), axis=0, keepdims=True)
        s1_ref[h] = s1
        s2_ref[h] = s2
        stat_ref[h] = jnp.concatenate([top[PEER_TOPK - 1:PEER_TOPK], v1[0:1] + jnp.log(z), v2[0:1],
                                       jnp.zeros((5, tm), F32)], axis=0)


def peer_route(q, keys, tm):
    m = q.shape[0]
    sc = jax.ShapeDtypeStruct((PEER_HEADS, PEER_N_KEYS, m), F32)
    sc_spec = pl.BlockSpec((PEER_HEADS, PEER_N_KEYS, tm), lambda i: (0, 0, i))
    return pl.pallas_call(
        _peer_route_kernel,
        grid=(m // tm,),
        in_specs=[pl.BlockSpec((tm, PEER_HEADS * PEER_KEY_DIM), lambda i: (i, 0)),
                  pl.BlockSpec((PEER_HEADS, 2, PEER_N_KEYS, PEER_HALF), lambda i: (0, 0, 0, 0))],
        out_specs=[sc_spec, sc_spec, pl.BlockSpec((PEER_HEADS, 8, tm), lambda i: (0, 0, i))],
        out_shape=[sc, sc, jax.ShapeDtypeStruct((PEER_HEADS, 8, m), F32)],
        compiler_params=_params("parallel"),
        name="peer_route",
    )(q, keys)


def _peer_expert_kernel(x_ref, u_ref, v_ref, s1_ref, s2_ref, stat_ref, o_ref, e1_s, e2_s, p_s, *, n1):
    j = pl.program_id(1)
    tm = x_ref.shape[0]

    @pl.when(j == 0)
    def _():
        for h in range(PEER_HEADS):
            e1_s[h] = jnp.exp(s1_ref[h] - stat_ref[h, 1:2, :])
            e2_s[h] = jnp.exp(s2_ref[h] - stat_ref[h, 2:3, :])
        o_ref[...] = jnp.zeros(o_ref.shape, F32)

    ht = _nt(u_ref[...], x_ref[...])
    act = 0.5 * ht * (1.0 + lax.erf(ht * (1.0 / math.sqrt(2.0))))
    for c in range(n1):
        i1 = j * n1 + c
        w = jnp.zeros((PEER_N_KEYS, tm), F32)
        for h in range(PEER_HEADS):
            pair = s1_ref[h, pl.ds(i1, 1), :] + s2_ref[h]
            w = w + jnp.where(pair >= stat_ref[h, 0:1, :], e1_s[h, pl.ds(i1, 1), :] * e2_s[h], 0.0)
        p_s[c * PEER_N_KEYS:(c + 1) * PEER_N_KEYS, :] = (w * act[c * PEER_N_KEYS:(c + 1) * PEER_N_KEYS]).astype(BF16)
    o_ref[...] += _tn(p_s[...], v_ref[...])


def peer_experts(xn, u, v, s1, s2, stat, tm, n1=2):
    m, d = xn.shape
    te = n1 * PEER_N_KEYS
    sc_spec = pl.BlockSpec((PEER_HEADS, PEER_N_KEYS, tm), lambda i, j: (0, 0, i))
    tab_spec = pl.BlockSpec((te, d), lambda i, j: (j, 0))
    return pl.pallas_call(
        functools.partial(_peer_expert_kernel, n1=n1),
        grid=(m // tm, PEER_EXPERTS // te),
        in_specs=[pl.BlockSpec((tm, d), lambda i, j: (i, 0)), tab_spec, tab_spec, sc_spec, sc_spec,
                  pl.BlockSpec((PEER_HEADS, 8, tm), lambda i, j: (0, 0, i))],
        out_specs=pl.BlockSpec((tm, d), lambda i, j: (i, 0)),
        out_shape=jax.ShapeDtypeStruct((m, d), F32),
        scratch_shapes=[pltpu.VMEM((PEER_HEADS, PEER_N_KEYS, tm), F32), pltpu.VMEM((PEER_HEADS, PEER_N_KEYS, tm), F32),
                        pltpu.VMEM((te, tm), BF16)],
        compiler_params=_params("parallel", "arbitrary"),
        name="peer_experts",
    )(xn, u, v, s1, s2, stat)


def peer(xn, w_q, keys, u_b, v_b):
    m = xn.shape[0]
    tm = _row_tile(m, 512)
    q = matmul([xn], w_q, 0, w_q.shape[1])
    s1, s2, stat = peer_route(q, keys, min(tm, 256))
    return peer_experts(xn, u_b, v_b, s1, s2, stat, tm)


def kernel(x_prompt, x_sample, mem_prompt, cache_k, cache_v, page_table, state_ssm, state_conv, cache_mem_k, cache_mem_v, norm_mix_g, norm_ffn_g, final_norm_g, a_w_in, a_conv_w, a_conv_b, a_dt_bias, a_A_log, a_D, a_norm_g, a_w_out, kv_norm_g, w_k, w_v, rel_bias, b_w_in, b_w_out, mem_norm_g, w_mem_k, w_mem_v, peer_w_q, peer_keys, peer_u, peer_v):
    depth = norm_mix_g.shape[0]
    assert depth == 2 and a_w_in.shape[0] == 1 and b_w_in.shape[0] == 1
    bp, seq, d = x_prompt.shape
    bs, n_new, _ = x_sample.shape
    rel_bias_t = rel_bias.T
    peer_u_b = peer_u.astype(BF16)
    peer_v_b = peer_v.astype(BF16)
    w_dt = a_w_in[0][:, SSM_ZX_W:SSM_ZX_W + SSM_HEADS]
    w_qm = a_w_in[0][:, SSM_ZX_W + SSM_HEADS:]

    mem_rows = mem_prompt.reshape(bp * MEM_LEN, d)
    mem_k_l, mem_v_l = [], []
    for l in range(depth):
        (mn,) = add_norm([mem_rows], [mem_norm_g[l]], [BF16], emit_sum=False)
        mem_k_l.append(matmul([mn], w_mem_k[l], 0, MEM_W).reshape(bp, MEM_LEN, MEM_HEADS, MEM_HEAD_DIM))
        mem_v_l.append(matmul([mn], w_mem_v[l], 0, MEM_W).reshape(bp, MEM_LEN, MEM_HEADS, MEM_HEAD_DIM))
    mem_k_prompt = jnp.stack(mem_k_l)
    mem_v_prompt = jnp.stack(mem_v_l)

    def run(x, length, mem_k, mem_v, conv0, ssm0, paged):
        nb = x.shape[0] // length
        (h,) = add_norm([x], [norm_mix_g[0]], [BF16], emit_sum=False)
        zx = matmul([h], a_w_in[0], 0, SSM_ZX_W)
        dt_raw = matmul([h], w_dt, 0, SSM_HEADS)
        qm = matmul([h], w_qm, 0, MEM_W)
        y_tok, conv_new, ssm_new = mamba(zx, dt_raw, conv0, ssm0, a_conv_w[0], a_conv_b[0], a_dt_bias[0],
                                         a_A_log[0], a_D[0], a_norm_g[0], length)
        o_mem = mem_attn(qm, 0, mem_k[0], mem_v[0], length)
        x = matmul([y_tok, o_mem], a_w_out[0], 0, d, res=x)
        (hf,) = add_norm([x], [norm_ffn_g[0]], [BF16], emit_sum=False)
        ffn = peer(hf, peer_w_q[0], peer_keys[0], peer_u_b[0], peer_v_b[0])
        x, kv_in, h = add_norm([x, ffn], [kv_norm_g, norm_mix_g[1]], [BF16, BF16], emit_sum=True)
        k_new = matmul([kv_in], w_k, 0, KV_W)
        v_new = matmul([kv_in], w_v, 0, KV_W)
        p = matmul([h], b_w_in[0], 0, ATT_HEADS * ATT_HEAD_DIM + MEM_W)
        if paged:
            y_tok = moba_sample(p, k_new, v_new, cache_k, cache_v, page_table, rel_bias_t, length)
        else:
            y_tok = moba_prompt(p, k_new, v_new, rel_bias_t, length)
        o_mem = mem_attn(p, ATT_HEADS * ATT_HEAD_DIM, mem_k[1], mem_v[1], length)
        x = matmul([y_tok, o_mem], b_w_out[0], 0, d, res=x)
        (hf,) = add_norm([x], [norm_ffn_g[1]], [BF16], emit_sum=False)
        ffn = peer(hf, peer_w_q[1], peer_keys[1], peer_u_b[1], peer_v_b[1])
        (y,) = add_norm([x, ffn], [final_norm_g], [F32], emit_sum=False)
        return (y.reshape(nb, length, d), conv_new[None], ssm_new[None],
                k_new.reshape(nb, length, KV_HEADS, ATT_HEAD_DIM), v_new.reshape(nb, length, KV_HEADS, ATT_HEAD_DIM))

    conv_zero = jnp.zeros((bp, SSM_CONV - 1, SSM_CONV_DIM), F32)
    ssm_zero = jnp.zeros((bp, SSM_HEADS, SSM_HEAD_DIM, SSM_STATE), F32)
    y_prompt, state_conv_prompt, state_ssm_prompt, k_prompt, v_prompt = run(
        x_prompt.reshape(bp * seq, d), seq, mem_k_prompt, mem_v_prompt, conv_zero, ssm_zero, False)
    y_sample, state_conv_sample, state_ssm_sample, k_sample, v_sample = run(
        x_sample.reshape(bs * n_new, d), n_new, cache_mem_k, cache_mem_v, state_conv[0], state_ssm[0], True)
    return (y_prompt, y_sample, state_ssm_prompt, state_conv_prompt, k_prompt, v_prompt, mem_k_prompt,
            mem_v_prompt, state_ssm_sample, state_conv_sample, k_sample, v_sample)
```
